```python
import math
import jax, jax.numpy as jnp
from jax import lax
import numpy as np

D_MODEL = 1024
BATCH = 16
SEQ = 2048
DEPTH = 2

CHUNK = 64
Q_BLOCK = 128
PLE_DIM = 256
D_MIX = D_MODEL
D_ATTN = D_MIX // 2
D_CONV = D_MIX - D_ATTN
N_HEADS_A = 4
HEAD_DIM_V = D_ATTN // N_HEADS_A
HEAD_DIM_QK = HEAD_DIM_V // 2
CONV_WIDTH = 3
N_IN_PARTS = 8
D_IN = 4 * D_ATTN + 4 * D_CONV
EPS = 1e-6
SUBLN_EPS = 1e-5

kernel_name = "hybrid_diffattn_shortconv_chunk_causal"


def rms_norm(x, gain, eps=EPS):
    xf = x.astype(jnp.float32)
    y = xf * lax.rsqrt(jnp.mean(xf * xf, axis=-1, keepdims=True) + eps)
    return (y * gain.astype(jnp.float32)).astype(x.dtype)


def alibi_slopes(n_heads):
    return jnp.exp2(-8.0 * jnp.arange(1, n_heads + 1, dtype=jnp.float32) / n_heads)


def diff_attention(q, k, v, lam, slopes):
    seq = q.shape[1]
    scale = HEAD_DIM_QK ** -0.5
    outs = []
    for q0 in range(0, seq, Q_BLOCK):
        k_end = q0 + Q_BLOCK
        t = jnp.arange(q0, k_end)
        s = jnp.arange(k_end)
        scores = jnp.einsum('bqhmd,bkhmd->bhmqk', q[:, q0:k_end], k[:, :k_end]).astype(jnp.float32) * scale
        dist = jnp.abs(t[:, None] - s[None, :]).astype(jnp.float32)
        allowed = (s[None, :] // CHUNK) <= (t[:, None] // CHUNK)
        bias = jnp.where(allowed[None], -slopes[:, None, None] * dist[None], -jnp.inf)
        probs = jax.nn.softmax(scores + bias[None, :, None], axis=-1)
        weights = probs[:, :, 0] - lam * probs[:, :, 1]
        outs.append(jnp.einsum('bhqk,bkhd->bqhd', weights.astype(v.dtype), v[:, :k_end]))
    return jnp.concatenate(outs, axis=1)


def setup_inputs(seed: int = 0) -> dict:
    key = jax.random.key(seed)
    ks = jax.random.split(key, 16)
    f32 = jnp.float32
    x = jax.random.normal(ks[0], (BATCH, SEQ, D_MODEL), f32)
    p = jax.random.normal(ks[1], (DEPTH, BATCH, SEQ, PLE_DIM), f32)
    pre_norm_gain = 1.0 + 0.02 * jax.random.normal(ks[2], (DEPTH, D_MODEL), f32)
    w_in = jax.random.normal(ks[3], (DEPTH, D_MODEL, D_IN), f32) * D_MODEL ** -0.5
    lambda_q1 = 0.1 * jax.random.normal(ks[4], (DEPTH, HEAD_DIM_QK), f32)
    lambda_k1 = 0.1 * jax.random.normal(ks[5], (DEPTH, HEAD_DIM_QK), f32)
    lambda_q2 = 0.1 * jax.random.normal(ks[6], (DEPTH, HEAD_DIM_QK), f32)
    lambda_k2 = 0.1 * jax.random.normal(ks[7], (DEPTH, HEAD_DIM_QK), f32)
    subln_gain = 1.0 + 0.02 * jax.random.normal(ks[8], (DEPTH, HEAD_DIM_V), f32)
    conv_w = jax.random.normal(ks[9], (DEPTH, CONV_WIDTH, D_CONV), f32) * CONV_WIDTH ** -0.5
    w_out = jax.random.normal(ks[10], (DEPTH, D_MIX, D_MODEL), f32) * D_MIX ** -0.5
    post_norm_gain = 1.0 + 0.02 * jax.random.normal(ks[11], (DEPTH, D_MODEL), f32)
    w_ple_proj = jax.random.normal(ks[12], (DEPTH, PLE_DIM, D_MODEL), f32) * PLE_DIM ** -0.5
    ple_norm_gain = 1.0 + 0.02 * jax.random.normal(ks[13], (DEPTH, D_MODEL), f32)
    w_ple_gate = jax.random.normal(ks[14], (DEPTH, D_MODEL, D_MODEL), f32) * D_MODEL ** -0.5
    return {"x": x, "p": p, "pre_norm_gain": pre_norm_gain, "w_in": w_in,
            "lambda_q1": lambda_q1, "lambda_k1": lambda_k1, "lambda_q2": lambda_q2,
            "lambda_k2": lambda_k2, "subln_gain": subln_gain, "conv_w": conv_w,
            "w_out": w_out, "post_norm_gain": post_norm_gain, "w_ple_proj": w_ple_proj,
            "ple_norm_gain": ple_norm_gain, "w_ple_gate": w_ple_gate}


def reference(x, p, pre_norm_gain, w_in, lambda_q1, lambda_k1, lambda_q2, lambda_k2,
              subln_gain, conv_w, w_out, post_norm_gain, w_ple_proj, ple_norm_gain, w_ple_gate):
    bsz, seq, _ = x.shape
    slopes = alibi_slopes(N_HEADS_A)
    for i in range(DEPTH):
        lambda_init = 0.8 - 0.6 * math.exp(-0.3 * i)
        h = rms_norm(x, pre_norm_gain[i])
        proj = h @ w_in[i]
        q, k, v, z_a, b_gate, c_gate, u, z_c = jnp.split(proj, N_IN_PARTS, axis=-1)

        q = q.reshape(bsz, seq, N_HEADS_A, 2, HEAD_DIM_QK)
        k = k.reshape(bsz, seq, N_HEADS_A, 2, HEAD_DIM_QK)
        v = v.reshape(bsz, seq, N_HEADS_A, HEAD_DIM_V)
        lam = (jnp.exp(jnp.sum((lambda_q1[i] * lambda_k1[i]).astype(jnp.float32)))
               - jnp.exp(jnp.sum((lambda_q2[i] * lambda_k2[i]).astype(jnp.float32)))
               + lambda_init)
        attn = diff_attention(q, k, v, lam, slopes)
        attn = rms_norm(attn, subln_gain[i], SUBLN_EPS) * (1.0 - lambda_init)
        attn = attn.reshape(bsz, seq, D_ATTN) * jax.nn.silu(z_a)

        g = c_gate * u
        g_pad = jnp.pad(g, ((0, 0), (CONV_WIDTH - 1, 0), (0, 0)))
        conv = conv_w[i, 0] * g_pad[:, 0:seq]
        for j in range(1, CONV_WIDTH):
            conv = conv + conv_w[i, j] * g_pad[:, j:j + seq]
        conv_out = b_gate * conv * jax.nn.silu(z_c)

        mixed = jnp.concatenate([attn, conv_out], axis=-1) @ w_out[i]
        x = x + rms_norm(mixed, post_norm_gain[i])

        e = rms_norm(p[i] @ w_ple_proj[i], ple_norm_gain[i])
        x = x + jax.nn.sigmoid(x @ w_ple_gate[i]) * e
    return x
```

```python
import functools
import math

import numpy as np
import jax
import jax.numpy as jnp
from jax import lax
from jax.experimental import pallas as pl
from jax.experimental.pallas import tpu as pltpu

F32 = jnp.float32
BF16 = jnp.bfloat16

CHUNK = 64
N_HEADS = 4
HEAD_DIM_V = 128
HEAD_DIM_QK = 64
PART = 512
CONV_WIDTH = 3
EPS = 1e-6
SUBLN_EPS = 1e-5
LOG2E = 1.4426950408889634
MASK_VALUE = -1e30

LANES = 128
SUBLANES = 8
VMEM_LIMIT_BYTES = 56 * 1024 * 1024

ROW_TILE = 512
ATT_TILE = 256
N_SPLIT = 4


def _rms(y, eps):
    return y * lax.rsqrt(jnp.mean(y * y, axis=-1, keepdims=True) + eps)


def _silu(z):
    return z * jax.nn.sigmoid(z)


def _in_proj_kernel(tiles_per_seq, x_ref, gain_ref, w_ref, cw_ref,
                    q_ref, k_ref, v_ref, ga_ref, co_ref, carry_ref):
    tm = x_ref.shape[0]
    h = (_rms(x_ref[...], EPS) * gain_ref[...]).astype(BF16)

    def part(n):
        return jnp.dot(h, w_ref[:, n * PART:(n + 1) * PART], preferred_element_type=F32)

    q_ref[...] = (part(0) * (LOG2E * HEAD_DIM_QK ** -0.5)).astype(BF16)
    k_ref[...] = part(1).astype(BF16)
    v_ref[...] = part(2).astype(BF16)
    ga_ref[...] = _silu(part(3)).astype(BF16)

    b_gate = part(4)
    g = part(5) * part(6)

    @pl.when(pl.program_id(0) % tiles_per_seq == 0)
    def _():
        carry_ref[...] = jnp.zeros_like(carry_ref)

    row = lax.broadcasted_iota(jnp.int32, g.shape, 0)
    prev1 = carry_ref[SUBLANES - 1:SUBLANES, :]
    prev2 = carry_ref[SUBLANES - 2:SUBLANES - 1, :]
    g1 = jnp.where(row == 0, prev1, pltpu.roll(g, 1, axis=0))
    g2 = pltpu.roll(g, 2, axis=0)
    g2 = jnp.where(row == 0, prev2, jnp.where(row == 1, prev1, g2))
    cw = cw_ref[...]
    conv = cw[0:1, :] * g2 + cw[1:2, :] * g1 + cw[2:3, :] * g
    carry_ref[...] = g[tm - SUBLANES:, :]
    co_ref[...] = (b_gate * conv * _silu(part(7))).astype(BF16)


def _in_proj(x2d, gain, w_bf16, conv_w, seq):
    n_rows, d_model = x2d.shape
    tm = ROW_TILE
    assert n_rows % tm == 0 and seq % tm == 0
    row_spec = lambda width: pl.BlockSpec((tm, width), lambda i: (i, 0))
    full = lambda a: pl.BlockSpec(a.shape, lambda i: (0,) * a.ndim)
    out = jax.ShapeDtypeStruct((n_rows, PART), BF16)
    return pl.pallas_call(
        functools.partial(_in_proj_kernel, seq // tm),
        grid=(n_rows // tm,),
        in_specs=[row_spec(d_model), full(gain), full(w_bf16), full(conv_w)],
        out_specs=[row_spec(PART)] * 5,
        out_shape=[out] * 5,
        scratch_shapes=[pltpu.VMEM((SUBLANES, PART), F32)],
        compiler_params=pltpu.CompilerParams(
            dimension_semantics=("arbitrary",), vmem_limit_bytes=VMEM_LIMIT_BYTES),
        name="in_proj",
    )(x2d, gain, w_bf16, conv_w)


def _split_bf16(x64, n):
    parts, r = [], np.array(x64, dtype=np.float64)
    for _ in range(n):
        p = r.astype(np.float32).astype(BF16)
        parts.append(p)
        r = r - p.astype(np.float64)
    return parts


@functools.lru_cache(maxsize=None)
def _alibi_tables(seq, tile):
    slopes = 2.0 ** (-8.0 * np.arange(1, N_HEADS + 1) / N_HEADS)
    pos = np.arange(seq, dtype=np.float64)
    qf = np.zeros((N_HEADS, seq, LANES), dtype=BF16)
    kf = np.zeros((N_HEADS, seq, LANES), dtype=BF16)
    r = np.arange(tile)
    diag = np.zeros((N_HEADS, tile, tile), dtype=np.float32)
    for h, m in enumerate(slopes):
        for n, piece in enumerate(_split_bf16(-m * LOG2E * pos, N_SPLIT)):
            qf[h, :, n] = piece
        qf[h, :, N_SPLIT:2 * N_SPLIT] = 1.0
        kf[h, :, 0:N_SPLIT] = 1.0
        for n, piece in enumerate(_split_bf16(m * LOG2E * pos, N_SPLIT)):
            kf[h, :, N_SPLIT + n] = piece
        ahead = r[None, :] - r[:, None]
        same_chunk = (r[None, :] // CHUNK) == (r[:, None] // CHUNK)
        later_chunk = (r[None, :] // CHUNK) > (r[:, None] // CHUNK)
        d = np.where((ahead > 0) & same_chunk, -2.0 * m * LOG2E * ahead, 0.0)
        diag[h] = np.where(later_chunk, MASK_VALUE, d).astype(np.float32)
    return qf, kf, diag


def _diff_attn_kernel(lambda_init, q_ref, k_ref, v_ref, ga_ref, qf_ref, kf_ref, diag_ref,
                      lam_ref, sg_ref, o_ref,
                      qaug_ref, kaug_ref, vaug_ref, s_ref, mrun_ref, acc_ref):
    tq = q_ref.shape[0]
    tk = diag_ref.shape[1]
    i = pl.program_id(2)

    @pl.when(i == 0)
    def _():
        kaug_ref[:, :LANES] = k_ref[...]
        kaug_ref[:, LANES:] = kf_ref[...]
        vaug_ref[:, :LANES] = v_ref[...]
        vaug_ref[:, LANES:] = jnp.ones(v_ref.shape, BF16)

    q = q_ref[...]
    lane = lax.broadcasted_iota(jnp.int32, q.shape, 1)
    zero = jnp.zeros_like(q)
    qaug_ref[:tq, :LANES] = jnp.where(lane < HEAD_DIM_QK, q, zero)
    qaug_ref[tq:, :LANES] = jnp.where(lane >= HEAD_DIM_QK, q, zero)
    qaug_ref[:tq, LANES:] = qf_ref[...]
    qaug_ref[tq:, LANES:] = qf_ref[...]

    def scores(j):
        kt = kaug_ref[pl.ds(pl.multiple_of(j * tk, tk), tk), :]
        return lax.dot_general(qaug_ref[...], kt, (((1,), (1,)), ((), ())),
                               preferred_element_type=F32)

    def fold_max(s):
        m = s[:, :LANES]
        for c in range(1, tk // LANES):
            m = jnp.maximum(m, s[:, c * LANES:(c + 1) * LANES])
        return m

    mrun_ref[...] = jnp.full(mrun_ref.shape, -jnp.inf, F32)

    def pass1(j, carry):
        s = scores(j)
        s_ref[j] = s
        mrun_ref[...] = jnp.maximum(mrun_ref[...], fold_max(s))
        return carry

    lax.fori_loop(0, i, pass1, 0)
    d = diag_ref[...]
    s = scores(i) + jnp.concatenate([d, d], axis=0)
    s_ref[i] = s
    m = jnp.max(jnp.maximum(mrun_ref[...], fold_max(s)), axis=-1, keepdims=True)

    acc_ref[...] = jnp.zeros_like(acc_ref)

    def pass2(j, carry):
        e = jnp.exp2(s_ref[j] - m).astype(BF16)
        vt = vaug_ref[pl.ds(pl.multiple_of(j * tk, tk), tk), :]
        acc_ref[...] += jnp.dot(e, vt, preferred_element_type=F32)
        return carry

    lax.fori_loop(0, i + 1, pass2, 0)

    lp = lam_ref[...]
    lam = (jnp.exp(jnp.sum(lp[0:1] * lp[1:2], axis=-1, keepdims=True))
           - jnp.exp(jnp.sum(lp[2:3] * lp[3:4], axis=-1, keepdims=True)) + lambda_init)
    acc = acc_ref[...]
    o = acc[:tq, :LANES] / acc[:tq, LANES:] - lam * (acc[tq:, :LANES] / acc[tq:, LANES:])
    o = _rms(o, SUBLN_EPS) * (sg_ref[...] * (1.0 - lambda_init))
    o_ref[...] = (o * ga_ref[...].astype(F32)).astype(BF16)


def _diff_attn(q, k, v, ga, lam_params, subln_gain, lambda_init):
    bsz, seq, _ = q.shape
    t = ATT_TILE
    assert seq % t == 0 and t % CHUNK == 0
    qf, kf, diag = _alibi_tables(seq, t)
    n_tiles = seq // t
    tile_spec = pl.BlockSpec((None, t, LANES), lambda b, h, i: (b, i, h))
    seq_spec = pl.BlockSpec((None, seq, LANES), lambda b, h, i: (b, 0, h))
    full = lambda a: pl.BlockSpec(a.shape, lambda b, h, i: (0,) * a.ndim)
    return pl.pallas_call(
        functools.partial(_diff_attn_kernel, lambda_init),
        grid=(bsz, N_HEADS, n_tiles),
        in_specs=[tile_spec, seq_spec, seq_spec, tile_spec,
                  pl.BlockSpec((None, t, LANES), lambda b, h, i: (h, i, 0)),
                  pl.BlockSpec((None, seq, LANES), lambda b, h, i: (h, 0, 0)),
                  pl.BlockSpec((None, t, t), lambda b, h, i: (h, 0, 0)),
                  full(lam_params), full(subln_gain)],
        out_specs=tile_spec,
        out_shape=jax.ShapeDtypeStruct((bsz, seq, N_HEADS * HEAD_DIM_V), BF16),
        scratch_shapes=[pltpu.VMEM((2 * t, 2 * LANES), BF16),
                        pltpu.VMEM((seq, 2 * LANES), BF16),
                        pltpu.VMEM((seq, 2 * LANES), BF16),
                        pltpu.VMEM((n_tiles, 2 * t, t), F32),
                        pltpu.VMEM((2 * t, LANES), F32),
                        pltpu.VMEM((2 * t, 2 * LANES), F32)],
        compiler_params=pltpu.CompilerParams(
            dimension_semantics=("arbitrary", "arbitrary", "arbitrary"),
            vmem_limit_bytes=VMEM_LIMIT_BYTES),
        name="diff_attn",
    )(q, k, v, ga, qf, kf, diag, lam_params, subln_gain)


def _out_mix_kernel(a_ref, c_ref, x_ref, p_ref, wo_ref, pg_ref, wp_ref, eg_ref, wg_ref, o_ref):
    d_attn = a_ref.shape[1]
    mixed = (jnp.dot(a_ref[...], wo_ref[:d_attn, :], preferred_element_type=F32)
             + jnp.dot(c_ref[...], wo_ref[d_attn:, :], preferred_element_type=F32))
    x1 = x_ref[...] + _rms(mixed, EPS) * pg_ref[...]
    e = _rms(jnp.dot(p_ref[...].astype(BF16), wp_ref[...], preferred_element_type=F32), EPS) * eg_ref[...]
    gate = jax.nn.sigmoid(jnp.dot(x1.astype(BF16), wg_ref[...], preferred_element_type=F32))
    o_ref[...] = x1 + gate * e


def _out_mix(attn2d, conv2d, x2d, p2d, w_out, post_gain, w_ple, ple_gain, w_gate):
    n_rows, d_model = x2d.shape
    tm = ROW_TILE
    row_spec = lambda a: pl.BlockSpec((tm, a.shape[1]), lambda i: (i, 0))
    full = lambda a: pl.BlockSpec(a.shape, lambda i: (0,) * a.ndim)
    args = (attn2d, conv2d, x2d, p2d, w_out, post_gain, w_ple, ple_gain, w_gate)
    return pl.pallas_call(
        _out_mix_kernel,
        grid=(n_rows // tm,),
        in_specs=[row_spec(a) for a in args[:4]] + [full(a) for a in args[4:]],
        out_specs=pl.BlockSpec((tm, d_model), lambda i: (i, 0)),
        out_shape=jax.ShapeDtypeStruct((n_rows, d_model), F32),
        compiler_params=pltpu.CompilerParams(
            dimension_semantics=("arbitrary",), vmem_limit_bytes=VMEM_LIMIT_BYTES),
        name="out_mix",
    )(*args)


def kernel(x, p, pre_norm_gain, w_in, lambda_q1, lambda_k1, lambda_q2, lambda_k2, subln_gain,
           conv_w, w_out, post_norm_gain, w_ple_proj, ple_norm_gain, w_ple_gate):
    bsz, seq, d_model = x.shape
    depth = p.shape[0]
    n_rows = bsz * seq
    x2d = x.reshape(n_rows, d_model)
    for i in range(depth):
        lambda_init = 0.8 - 0.6 * math.exp(-0.3 * i)
        q, k, v, ga, co = _in_proj(x2d, pre_norm_gain[i][None, :], w_in[i].astype(BF16), conv_w[i], seq)
        to3d = lambda a: a.reshape(bsz, seq, PART)
        lam_params = jnp.stack([lambda_q1[i], lambda_k1[i], lambda_q2[i], lambda_k2[i]])
        attn = _diff_attn(to3d(q), to3d(k), to3d(v), to3d(ga), lam_params,
                          subln_gain[i][None, :], lambda_init)
        x2d = _out_mix(attn.reshape(n_rows, PART), co, x2d, p[i].reshape(n_rows, -1),
                       w_out[i].astype(BF16), post_norm_gain[i][None, :],
                       w_ple_proj[i].astype(BF16), ple_norm_gain[i][None, :],
                       w_ple_gate[i].astype(BF16))
    return x2d.reshape(bsz, seq, d_model)
```

```python
import functools
import math

import numpy as np
import jax
import jax.numpy as jnp
from jax import lax
from jax.experimental import pallas as pl
from jax.experimental.pallas import tpu as pltpu

F32 = jnp.float32
BF16 = jnp.bfloat16

CHUNK = 64
N_HEADS = 4
HEAD_DIM_V = 128
HEAD_DIM_QK = 64
PART = 512
CONV_WIDTH = 3
EPS = 1e-6
SUBLN_EPS = 1e-5
LOG2E = 1.4426950408889634
MASK_VALUE = -1e30

LANES = 128
SUBLANES = 8
VMEM_LIMIT_BYTES = 56 * 1024 * 1024

ROW_TILE = 512
ATT_TILE = 256
N_SPLIT = 4


def _rms(y, eps):
    return y * lax.rsqrt(jnp.mean(y * y, axis=-1, keepdims=True) + eps)


def _silu(z):
    return z * jax.nn.sigmoid(z)


def _in_proj_kernel(tiles_per_seq, x_ref, gain_ref, w_ref, cw_ref,
                    q_ref, k_ref, v_ref, ga_ref, co_ref, carry_ref):
    tm = x_ref.shape[0]
    h = (_rms(x_ref[...], EPS) * gain_ref[...]).astype(BF16)

    def part(n):
        return jnp.dot(h, w_ref[:, n * PART:(n + 1) * PART], preferred_element_type=F32)

    q_ref[...] = (part(0) * (LOG2E * HEAD_DIM_QK ** -0.5)).astype(BF16)
    k_ref[...] = part(1).astype(BF16)
    v_ref[...] = part(2).astype(BF16)
    ga_ref[...] = _silu(part(3)).astype(BF16)

    b_gate = part(4)
    g = part(5) * part(6)

    @pl.when(pl.program_id(0) % tiles_per_seq == 0)
    def _():
        carry_ref[...] = jnp.zeros_like(carry_ref)

    row = lax.broadcasted_iota(jnp.int32, g.shape, 0)
    prev1 = carry_ref[SUBLANES - 1:SUBLANES, :]
    prev2 = carry_ref[SUBLANES - 2:SUBLANES - 1, :]
    g1 = jnp.where(row == 0, prev1, pltpu.roll(g, 1, axis=0))
    g2 = pltpu.roll(g, 2, axis=0)
    g2 = jnp.where(row == 0, prev2, jnp.where(row == 1, prev1, g2))
    cw = cw_ref[...]
    conv = cw[0:1, :] * g2 + cw[1:2, :] * g1 + cw[2:3, :] * g
    carry_ref[...] = g[tm - SUBLANES:, :]
    co_ref[...] = (b_gate * conv * _silu(part(7))).astype(BF16)


def _in_proj(x2d, gain, w_bf16, conv_w, seq):
    n_rows, d_model = x2d.shape
    tm = ROW_TILE
    assert n_rows % tm == 0 and seq % tm == 0
    row_spec = lambda width: pl.BlockSpec((tm, width), lambda i: (i, 0))
    full = lambda a: pl.BlockSpec(a.shape, lambda i: (0,) * a.ndim)
    out = jax.ShapeDtypeStruct((n_rows, PART), BF16)
    return pl.pallas_call(
        functools.partial(_in_proj_kernel, seq // tm),
        grid=(n_rows // tm,),
        in_specs=[row_spec(d_model), full(gain), full(w_bf16), full(conv_w)],
        out_specs=[row_spec(PART)] * 5,
        out_shape=[out] * 5,
        scratch_shapes=[pltpu.VMEM((SUBLANES, PART), F32)],
        compiler_params=pltpu.CompilerParams(
            dimension_semantics=("arbitrary",), vmem_limit_bytes=VMEM_LIMIT_BYTES),
        name="in_proj",
    )(x2d, gain, w_bf16, conv_w)


def _split_bf16(x64, n):
    parts, r = [], np.array(x64, dtype=np.float64)
    for _ in range(n):
        p = r.astype(np.float32).astype(BF16)
        parts.append(p)
        r = r - p.astype(np.float64)
    return parts


@functools.lru_cache(maxsize=None)
def _alibi_tables(seq, tile):
    slopes = 2.0 ** (-8.0 * np.arange(1, N_HEADS + 1) / N_HEADS)
    pos = np.arange(seq, dtype=np.float64)
    qf = np.zeros((N_HEADS, seq, LANES), dtype=BF16)
    kf = np.zeros((N_HEADS, seq, LANES), dtype=BF16)
    r = np.arange(tile)
    diag = np.zeros((N_HEADS, tile, tile), dtype=np.float32)
    for h, m in enumerate(slopes):
        for n, piece in enumerate(_split_bf16(-m * LOG2E * pos, N_SPLIT)):
            qf[h, :, n] = piece
        qf[h, :, N_SPLIT:2 * N_SPLIT] = 1.0
        kf[h, :, 0:N_SPLIT] = 1.0
        for n, piece in enumerate(_split_bf16(m * LOG2E * pos, N_SPLIT)):
            kf[h, :, N_SPLIT + n] = piece
        ahead = r[None, :] - r[:, None]
        same_chunk = (r[None, :] // CHUNK) == (r[:, None] // CHUNK)
        later_chunk = (r[None, :] // CHUNK) > (r[:, None] // CHUNK)
        d = np.where((ahead > 0) & same_chunk, -2.0 * m * LOG2E * ahead, 0.0)
        diag[h] = np.where(later_chunk, MASK_VALUE, d).astype(np.float32)
    return qf, kf, diag


def _diff_attn_kernel(lambda_init, q_ref, k_ref, v_ref, ga_ref, qf_ref, kf_ref, diag_ref,
                      lam_ref, sg_ref, o_ref, s_ref, e_ref):
    seq = q_ref.shape[0]
    t = diag_ref.shape[0]
    n_tiles = seq // t
    tile = lambda ref, j: ref[j * t:(j + 1) * t, :]

    lp = lam_ref[...]
    lam = (jnp.exp(jnp.sum(lp[0:1] * lp[1:2], axis=-1, keepdims=True))
           - jnp.exp(jnp.sum(lp[2:3] * lp[3:4], axis=-1, keepdims=True)) + lambda_init)
    out_gain = sg_ref[...] * (1.0 - lambda_init)
    d = diag_ref[...]
    dd = jnp.concatenate([d, d], axis=0)
    ones = jnp.ones((t, LANES), BF16)

    def fold_max(s):
        m = s[:, :LANES]
        for c in range(1, t // LANES):
            m = jnp.maximum(m, s[:, c * LANES:(c + 1) * LANES])
        return m

    for i in range(n_tiles):
        slot = i % 2
        q = tile(q_ref, i)
        qf = tile(qf_ref, i)
        lane = lax.broadcasted_iota(jnp.int32, q.shape, 1)
        zero = jnp.zeros_like(q)
        qaug = jnp.concatenate(
            [jnp.concatenate([jnp.where(lane < HEAD_DIM_QK, q, zero), qf], axis=1),
             jnp.concatenate([jnp.where(lane >= HEAD_DIM_QK, q, zero), qf], axis=1)], axis=0)

        m = None
        for j in range(i + 1):
            kt = jnp.concatenate([tile(k_ref, j), tile(kf_ref, j)], axis=1)
            s = lax.dot_general(qaug, kt, (((1,), (1,)), ((), ())), preferred_element_type=F32)
            if j == i:
                s = s + dd
            s_ref[slot, :, j * t:(j + 1) * t] = s
            f = fold_max(s)
            m = f if m is None else jnp.maximum(m, f)
        m = jnp.max(m, axis=-1, keepdims=True)

        for j in range(i + 1):
            e_ref[slot, :, j * t:(j + 1) * t] = jnp.exp2(s_ref[slot, :, j * t:(j + 1) * t] - m).astype(BF16)
        k_end = (i + 1) * t
        vaug = jnp.concatenate([v_ref[:k_end, :], jnp.concatenate([ones] * (i + 1), axis=0)], axis=1)
        acc = jnp.dot(e_ref[slot, :, :k_end], vaug, preferred_element_type=F32)

        o = acc[:t, :LANES] / acc[:t, LANES:] - lam * (acc[t:, :LANES] / acc[t:, LANES:])
        o = _rms(o, SUBLN_EPS) * out_gain
        o_ref[i * t:(i + 1) * t, :] = (o * tile(ga_ref, i).astype(F32)).astype(BF16)


def _diff_attn(q, k, v, ga, lam_params, subln_gain, lambda_init):
    bsz, seq, _ = q.shape
    t = ATT_TILE
    assert seq % t == 0 and t % CHUNK == 0
    qf, kf, diag = _alibi_tables(seq, t)
    head_spec = pl.BlockSpec((None, seq, LANES), lambda b, h: (b, 0, h))
    table_spec = pl.BlockSpec((None, seq, LANES), lambda b, h: (h, 0, 0))
    full = lambda a: pl.BlockSpec(a.shape, lambda b, h: (0,) * a.ndim)
    return pl.pallas_call(
        functools.partial(_diff_attn_kernel, lambda_init),
        grid=(bsz, N_HEADS),
        in_specs=[head_spec, head_spec, head_spec, head_spec, table_spec, table_spec,
                  pl.BlockSpec((None, t, t), lambda b, h: (h, 0, 0)),
                  full(lam_params), full(subln_gain)],
        out_specs=head_spec,
        out_shape=jax.ShapeDtypeStruct((bsz, seq, N_HEADS * HEAD_DIM_V), BF16),
        scratch_shapes=[pltpu.VMEM((2, 2 * t, seq), F32),
                        pltpu.VMEM((2, 2 * t, seq), BF16)],
        compiler_params=pltpu.CompilerParams(
            dimension_semantics=("arbitrary", "arbitrary"), vmem_limit_bytes=VMEM_LIMIT_BYTES),
        name="diff_attn",
    )(q, k, v, ga, qf, kf, diag, lam_params, subln_gain)


def _out_mix_kernel(a_ref, c_ref, x_ref, p_ref, wo_ref, pg_ref, wp_ref, eg_ref, wg_ref, o_ref):
    d_attn = a_ref.shape[1]
    mixed = (jnp.dot(a_ref[...], wo_ref[:d_attn, :], preferred_element_type=F32)
             + jnp.dot(c_ref[...], wo_ref[d_attn:, :], preferred_element_type=F32))
    x1 = x_ref[...] + _rms(mixed, EPS) * pg_ref[...]
    e = _rms(jnp.dot(p_ref[...].astype(BF16), wp_ref[...], preferred_element_type=F32), EPS) * eg_ref[...]
    gate = jax.nn.sigmoid(jnp.dot(x1.astype(BF16), wg_ref[...], preferred_element_type=F32))
    o_ref[...] = x1 + gate * e


def _out_mix(attn2d, conv2d, x2d, p2d, w_out, post_gain, w_ple, ple_gain, w_gate):
    n_rows, d_model = x2d.shape
    tm = ROW_TILE
    row_spec = lambda a: pl.BlockSpec((tm, a.shape[1]), lambda i: (i, 0))
    full = lambda a: pl.BlockSpec(a.shape, lambda i: (0,) * a.ndim)
    args = (attn2d, conv2d, x2d, p2d, w_out, post_gain, w_ple, ple_gain, w_gate)
    return pl.pallas_call(
        _out_mix_kernel,
        grid=(n_rows // tm,),
        in_specs=[row_spec(a) for a in args[:4]] + [full(a) for a in args[4:]],
        out_specs=pl.BlockSpec((tm, d_model), lambda i: (i, 0)),
        out_shape=jax.ShapeDtypeStruct((n_rows, d_model), F32),
        compiler_params=pltpu.CompilerParams(
            dimension_semantics=("arbitrary",), vmem_limit_bytes=VMEM_LIMIT_BYTES),
        name="out_mix",
    )(*args)


def kernel(x, p, pre_norm_gain, w_in, lambda_q1, lambda_k1, lambda_q2, lambda_k2, subln_gain,
           conv_w, w_out, post_norm_gain, w_ple_proj, ple_norm_gain, w_ple_gate):
    bsz, seq, d_model = x.shape
    depth = p.shape[0]
    n_rows = bsz * seq
    x2d = x.reshape(n_rows, d_model)
    for i in range(depth):
        lambda_init = 0.8 - 0.6 * math.exp(-0.3 * i)
        q, k, v, ga, co = _in_proj(x2d, pre_norm_gain[i][None, :], w_in[i].astype(BF16), conv_w[i], seq)
        to3d = lambda a: a.reshape(bsz, seq, PART)
        lam_params = jnp.stack([lambda_q1[i], lambda_k1[i], lambda_q2[i], lambda_k2[i]])
        attn = _diff_attn(to3d(q), to3d(k), to3d(v), to3d(ga), lam_params,
                          subln_gain[i][None, :], lambda_init)
        x2d = _out_mix(attn.reshape(n_rows, PART), co, x2d, p[i].reshape(n_rows, -1),
                       w_out[i].astype(BF16), post_norm_gain[i][None, :],
                       w_ple_proj[i].astype(BF16), ple_norm_gain[i][None, :],
                       w_ple_gate[i].astype(BF16))
    return x2d.reshape(bsz, seq, d_model)
```

```python
import functools
import math

import numpy as np
import jax
import jax.numpy as jnp
from jax import lax
from jax.experimental import pallas as pl
from jax.experimental.pallas import tpu as pltpu

F32 = jnp.float32
BF16 = jnp.bfloat16

CHUNK = 64
N_HEADS = 4
HEAD_DIM_V = 128
HEAD_DIM_QK = 64
PART = 512
CONV_WIDTH = 3
EPS = 1e-6
SUBLN_EPS = 1e-5
LOG2E = 1.4426950408889634
MASK_VALUE = -1e30

LANES = 128
SUBLANES = 8
VMEM_LIMIT_BYTES = 56 * 1024 * 1024

ROW_TILE = 1024
SUB_ROWS = 256
MIX_SUB_ROWS = 512
ATT_SLOTS = 2
ATT_TILE = 256
N_SPLIT = 4


def _rms(y, eps):
    return y * lax.rsqrt(jnp.mean(y * y, axis=-1, keepdims=True) + eps)


def _silu(z):
    return z * jax.nn.sigmoid(z)


def _in_proj_kernel(tiles_per_seq, x_ref, gain_ref, w_ref, cw_ref,
                    q_ref, k_ref, v_ref, ga_ref, co_ref, carry_ref):
    tm = x_ref.shape[0]
    n_sub = tm // SUB_ROWS
    sub = lambda r: slice(r * SUB_ROWS, (r + 1) * SUB_ROWS)

    @pl.when(pl.program_id(0) % tiles_per_seq == 0)
    def _():
        carry_ref[...] = jnp.zeros_like(carry_ref)

    def norm(r):
        return (_rms(x_ref[sub(r), :], EPS) * gain_ref[...]).astype(BF16)

    def project(r, h, tail):
        rows = sub(r)

        def part(n):
            return jnp.dot(h, w_ref[:, n * PART:(n + 1) * PART], preferred_element_type=F32)

        b_gate = part(4)
        g = part(5) * part(6)
        row = lax.broadcasted_iota(jnp.int32, g.shape, 0)
        prev1 = tail[SUBLANES - 1:SUBLANES, :]
        prev2 = tail[SUBLANES - 2:SUBLANES - 1, :]
        g1 = jnp.where(row == 0, prev1, pltpu.roll(g, 1, axis=0))
        g2 = pltpu.roll(g, 2, axis=0)
        g2 = jnp.where(row == 0, prev2, jnp.where(row == 1, prev1, g2))
        cw = cw_ref[...]
        conv = cw[0:1, :] * g2 + cw[1:2, :] * g1 + cw[2:3, :] * g
        co_ref[rows, :] = (b_gate * conv * _silu(part(7))).astype(BF16)
        ga_ref[rows, :] = _silu(part(3)).astype(BF16)
        q_ref[rows, :] = (part(0) * (LOG2E * HEAD_DIM_QK ** -0.5)).astype(BF16)
        k_ref[rows, :] = part(1).astype(BF16)
        v_ref[rows, :] = part(2).astype(BF16)
        return g[SUB_ROWS - SUBLANES:, :]

    tail = carry_ref[...]
    h = norm(0)
    for r in range(n_sub):
        h_next = norm(r + 1) if r + 1 < n_sub else None
        tail = project(r, h, tail)
        h = h_next
    carry_ref[...] = tail


def _in_proj(x2d, gain, w_bf16, conv_w, seq):
    n_rows, d_model = x2d.shape
    tm = ROW_TILE
    assert n_rows % tm == 0 and seq % tm == 0 and tm % SUB_ROWS == 0
    row_spec = lambda width: pl.BlockSpec((tm, width), lambda i: (i, 0))
    full = lambda a: pl.BlockSpec(a.shape, lambda i: (0,) * a.ndim)
    out = jax.ShapeDtypeStruct((n_rows, PART), BF16)
    return pl.pallas_call(
        functools.partial(_in_proj_kernel, seq // tm),
        grid=(n_rows // tm,),
        in_specs=[row_spec(d_model), full(gain), full(w_bf16), full(conv_w)],
        out_specs=[row_spec(PART)] * 5,
        out_shape=[out] * 5,
        scratch_shapes=[pltpu.VMEM((SUBLANES, PART), F32)],
        compiler_params=pltpu.CompilerParams(
            dimension_semantics=("arbitrary",), vmem_limit_bytes=VMEM_LIMIT_BYTES),
        name="in_proj",
    )(x2d, gain, w_bf16, conv_w)


def _split_bf16(x64, n):
    parts, r = [], np.array(x64, dtype=np.float64)
    for _ in range(n):
        p = r.astype(np.float32).astype(BF16)
        parts.append(p)
        r = r - p.astype(np.float64)
    return parts


@functools.lru_cache(maxsize=None)
def _alibi_tables(seq, tile):
    slopes = 2.0 ** (-8.0 * np.arange(1, N_HEADS + 1) / N_HEADS)
    pos = np.arange(seq, dtype=np.float64)
    qf = np.zeros((N_HEADS, seq, LANES), dtype=BF16)
    kf = np.zeros((N_HEADS, seq, LANES), dtype=BF16)
    r = np.arange(tile)
    diag = np.zeros((N_HEADS, tile, tile), dtype=np.float32)
    for h, m in enumerate(slopes):
        for n, piece in enumerate(_split_bf16(-m * LOG2E * pos, N_SPLIT)):
            qf[h, :, n] = piece
        qf[h, :, N_SPLIT:2 * N_SPLIT] = 1.0
        kf[h, :, 0:N_SPLIT] = 1.0
        for n, piece in enumerate(_split_bf16(m * LOG2E * pos, N_SPLIT)):
            kf[h, :, N_SPLIT + n] = piece
        ahead = r[None, :] - r[:, None]
        same_chunk = (r[None, :] // CHUNK) == (r[:, None] // CHUNK)
        later_chunk = (r[None, :] // CHUNK) > (r[:, None] // CHUNK)
        d = np.where((ahead > 0) & same_chunk, -2.0 * m * LOG2E * ahead, 0.0)
        diag[h] = np.where(later_chunk, MASK_VALUE, d).astype(np.float32)
    return qf, kf, diag


def _diff_attn_kernel(lambda_init, q_ref, k_ref, v_ref, ga_ref, qf_ref, kf_ref, diag_ref,
                      lam_ref, sg_ref, o_ref, s_ref, e_ref):
    seq = q_ref.shape[0]
    t = diag_ref.shape[0]
    n_tiles = seq // t
    tile = lambda ref, j: ref[j * t:(j + 1) * t, :]

    lp = lam_ref[...]
    lam = (jnp.exp(jnp.sum(lp[0:1] * lp[1:2], axis=-1, keepdims=True))
           - jnp.exp(jnp.sum(lp[2:3] * lp[3:4], axis=-1, keepdims=True)) + lambda_init)
    out_gain = sg_ref[...] * (1.0 - lambda_init)
    d = diag_ref[...]
    dd = jnp.concatenate([d, d], axis=0)
    ones = jnp.ones((t, LANES), BF16)

    def fold_max(s):
        m = s[:, :LANES]
        for c in range(1, t // LANES):
            m = jnp.maximum(m, s[:, c * LANES:(c + 1) * LANES])
        return m

    def pass1(i, slot):
        q = tile(q_ref, i)
        qf = tile(qf_ref, i)
        lane = lax.broadcasted_iota(jnp.int32, q.shape, 1)
        zero = jnp.zeros_like(q)
        qaug = jnp.concatenate(
            [jnp.concatenate([jnp.where(lane < HEAD_DIM_QK, q, zero), qf], axis=1),
             jnp.concatenate([jnp.where(lane >= HEAD_DIM_QK, q, zero), qf], axis=1)], axis=0)
        m = None
        for j in range(i + 1):
            kt = jnp.concatenate([tile(k_ref, j), tile(kf_ref, j)], axis=1)
            s = lax.dot_general(qaug, kt, (((1,), (1,)), ((), ())), preferred_element_type=F32)
            if j == i:
                s = s + dd
            s_ref[slot, :, j * t:(j + 1) * t] = s
            f = fold_max(s)
            m = f if m is None else jnp.maximum(m, f)
        return jnp.max(m, axis=-1, keepdims=True)

    def pass2(i, slot, m):
        for j in range(i + 1):
            e_ref[slot, :, j * t:(j + 1) * t] = jnp.exp2(s_ref[slot, :, j * t:(j + 1) * t] - m).astype(BF16)
        k_end = (i + 1) * t
        vaug = jnp.concatenate([v_ref[:k_end, :], jnp.concatenate([ones] * (i + 1), axis=0)], axis=1)
        acc = jnp.dot(e_ref[slot, :, :k_end], vaug, preferred_element_type=F32)
        o = acc[:t, :LANES] / acc[:t, LANES:] - lam * (acc[t:, :LANES] / acc[t:, LANES:])
        o = _rms(o, SUBLN_EPS) * out_gain
        o_ref[i * t:(i + 1) * t, :] = (o * tile(ga_ref, i).astype(F32)).astype(BF16)

    order = list(range(1, n_tiles, 2)) + list(range(n_tiles - 2 + n_tiles % 2, -1, -2))
    assert sorted(order) == list(range(n_tiles))
    row_max = {}
    for n in range(n_tiles + 1):
        if n < n_tiles:
            row_max[n] = pass1(order[n], n % ATT_SLOTS)
        if n >= 1:
            pass2(order[n - 1], (n - 1) % ATT_SLOTS, row_max.pop(n - 1))


def _diff_attn(q, k, v, ga, lam_params, subln_gain, lambda_init):
    bsz, seq, _ = q.shape
    t = ATT_TILE
    assert seq % t == 0 and t % CHUNK == 0
    qf, kf, diag = _alibi_tables(seq, t)
    head_spec = pl.BlockSpec((None, seq, LANES), lambda b, h: (b, 0, h))
    table_spec = pl.BlockSpec((None, seq, LANES), lambda b, h: (h, 0, 0))
    full = lambda a: pl.BlockSpec(a.shape, lambda b, h: (0,) * a.ndim)
    return pl.pallas_call(
        functools.partial(_diff_attn_kernel, lambda_init),
        grid=(bsz, N_HEADS),
        in_specs=[head_spec, head_spec, head_spec, head_spec, table_spec, table_spec,
                  pl.BlockSpec((None, t, t), lambda b, h: (h, 0, 0)),
                  full(lam_params), full(subln_gain)],
        out_specs=head_spec,
        out_shape=jax.ShapeDtypeStruct((bsz, seq, N_HEADS * HEAD_DIM_V), BF16),
        scratch_shapes=[pltpu.VMEM((ATT_SLOTS, 2 * t, seq), F32),
                        pltpu.VMEM((ATT_SLOTS, 2 * t, seq), BF16)],
        compiler_params=pltpu.CompilerParams(
            dimension_semantics=("arbitrary", "arbitrary"), vmem_limit_bytes=VMEM_LIMIT_BYTES),
        name="diff_attn",
    )(q, k, v, ga, qf, kf, diag, lam_params, subln_gain)


def _out_mix_kernel(a_ref, c_ref, x_ref, p_ref, wo_ref, pg_ref, wp_ref, eg_ref, wg_ref, o_ref):
    d_attn = a_ref.shape[1]
    n_sub = x_ref.shape[0] // MIX_SUB_ROWS
    sub = lambda r: slice(r * MIX_SUB_ROWS, (r + 1) * MIX_SUB_ROWS)

    def mix(r):
        rows = sub(r)
        mixed = (jnp.dot(a_ref[rows, :], wo_ref[:d_attn, :], preferred_element_type=F32)
                 + jnp.dot(c_ref[rows, :], wo_ref[d_attn:, :], preferred_element_type=F32))
        x1 = x_ref[rows, :] + _rms(mixed, EPS) * pg_ref[...]
        e = _rms(jnp.dot(p_ref[rows, :].astype(BF16), wp_ref[...], preferred_element_type=F32),
                 EPS) * eg_ref[...]
        return x1, e

    def gate(r, x1, e):
        g = jax.nn.sigmoid(jnp.dot(x1.astype(BF16), wg_ref[...], preferred_element_type=F32))
        o_ref[sub(r), :] = x1 + g * e

    cur = mix(0)
    for r in range(n_sub):
        nxt = mix(r + 1) if r + 1 < n_sub else None
        gate(r, *cur)
        cur = nxt


def _out_mix(attn2d, conv2d, x2d, p2d, w_out, post_gain, w_ple, ple_gain, w_gate):
    n_rows, d_model = x2d.shape
    tm = ROW_TILE
    row_spec = lambda a: pl.BlockSpec((tm, a.shape[1]), lambda i: (i, 0))
    full = lambda a: pl.BlockSpec(a.shape, lambda i: (0,) * a.ndim)
    args = (attn2d, conv2d, x2d, p2d, w_out, post_gain, w_ple, ple_gain, w_gate)
    return pl.pallas_call(
        _out_mix_kernel,
        grid=(n_rows // tm,),
        in_specs=[row_spec(a) for a in args[:4]] + [full(a) for a in args[4:]],
        out_specs=pl.BlockSpec((tm, d_model), lambda i: (i, 0)),
        out_shape=jax.ShapeDtypeStruct((n_rows, d_model), F32),
        compiler_params=pltpu.CompilerParams(
            dimension_semantics=("arbitrary",), vmem_limit_bytes=VMEM_LIMIT_BYTES),
        name="out_mix",
    )(*args)


def kernel(x, p, pre_norm_gain, w_in, lambda_q1, lambda_k1, lambda_q2, lambda_k2, subln_gain,
           conv_w, w_out, post_norm_gain, w_ple_proj, ple_norm_gain, w_ple_gate):
    bsz, seq, d_model = x.shape
    depth = p.shape[0]
    n_rows = bsz * seq
    x2d = x.reshape(n_rows, d_model)
    for i in range(depth):
        lambda_init = 0.8 - 0.6 * math.exp(-0.3 * i)
        q, k, v, ga, co = _in_proj(x2d, pre_norm_gain[i][None, :], w_in[i].astype(BF16), conv_w[i], seq)
        to3d = lambda a: a.reshape(bsz, seq, PART)
        lam_params = jnp.stack([lambda_q1[i], lambda_k1[i], lambda_q2[i], lambda_k2[i]])
        attn = _diff_attn(to3d(q), to3d(k), to3d(v), to3d(ga), lam_params,
                          subln_gain[i][None, :], lambda_init)
        x2d = _out_mix(attn.reshape(n_rows, PART), co, x2d, p[i].reshape(n_rows, -1),
                       w_out[i].astype(BF16), post_norm_gain[i][None, :],
                       w_ple_proj[i].astype(BF16), ple_norm_gain[i][None, :],
                       w_ple_gate[i].astype(BF16))
    return x2d.reshape(bsz, seq, d_model)
```

```python
import functools
import math

import numpy as np
import jax
import jax.numpy as jnp
from jax import lax
from jax.experimental import pallas as pl
from jax.experimental.pallas import tpu as pltpu

F32 = jnp.float32
BF16 = jnp.bfloat16

CHUNK = 64
N_HEADS = 4
HEAD_DIM_V = 128
HEAD_DIM_QK = 64
PART = 512
CONV_WIDTH = 3
EPS = 1e-6
SUBLN_EPS = 1e-5
LOG2E = 1.4426950408889634
MASK_VALUE = -1e30

LANES = 128
SUBLANES = 8
VMEM_LIMIT_BYTES = 56 * 1024 * 1024

ROW_TILE = 1024
SUB_ROWS = 256
MIX_SUB_ROWS = 512
ATT_SLOTS = 2
ATT_TILE = 256
N_SPLIT = 4


def _rms(y, eps):
    return y * lax.rsqrt(jnp.mean(y * y, axis=-1, keepdims=True) + eps)


def _silu(z):
    return z * jax.nn.sigmoid(z)


def _cast_weight(dst_ref, src_ref):
    for c in range(0, src_ref.shape[1], PART):
        dst_ref[:, c:c + PART] = src_ref[:, c:c + PART].astype(BF16)


def _in_proj_kernel(tiles_per_seq, x_ref, gain_ref, w32_ref, cw_ref,
                    q_ref, k_ref, v_ref, ga_ref, co_ref, w_ref, carry_ref):
    tm = x_ref.shape[0]
    n_sub = tm // SUB_ROWS
    sub = lambda r: slice(r * SUB_ROWS, (r + 1) * SUB_ROWS)

    @pl.when(pl.program_id(0) == 0)
    def _():
        _cast_weight(w_ref, w32_ref)

    @pl.when(pl.program_id(0) % tiles_per_seq == 0)
    def _():
        carry_ref[...] = jnp.zeros_like(carry_ref)

    def norm(r):
        return (_rms(x_ref[sub(r), :], EPS) * gain_ref[...]).astype(BF16)

    def project(r, h, tail):
        rows = sub(r)

        def part(n):
            return jnp.dot(h, w_ref[:, n * PART:(n + 1) * PART], preferred_element_type=F32)

        b_gate = part(4)
        g = part(5) * part(6)
        row = lax.broadcasted_iota(jnp.int32, g.shape, 0)
        prev1 = tail[SUBLANES - 1:SUBLANES, :]
        prev2 = tail[SUBLANES - 2:SUBLANES - 1, :]
        g1 = jnp.where(row == 0, prev1, pltpu.roll(g, 1, axis=0))
        g2 = pltpu.roll(g, 2, axis=0)
        g2 = jnp.where(row == 0, prev2, jnp.where(row == 1, prev1, g2))
        cw = cw_ref[...]
        conv = cw[0:1, :] * g2 + cw[1:2, :] * g1 + cw[2:3, :] * g
        co_ref[rows, :] = (b_gate * conv * _silu(part(7))).astype(BF16)
        ga_ref[rows, :] = _silu(part(3)).astype(BF16)
        q_ref[rows, :] = (part(0) * (LOG2E * HEAD_DIM_QK ** -0.5)).astype(BF16)
        k_ref[rows, :] = part(1).astype(BF16)
        v_ref[rows, :] = part(2).astype(BF16)
        return g[SUB_ROWS - SUBLANES:, :]

    tail = carry_ref[...]
    h = norm(0)
    for r in range(n_sub):
        h_next = norm(r + 1) if r + 1 < n_sub else None
        tail = project(r, h, tail)
        h = h_next
    carry_ref[...] = tail


def _layer_spec(a, layer, resident=False):
    return pl.BlockSpec((None,) + a.shape[1:], lambda *_: (layer,) + (0,) * (a.ndim - 1),
                        pipeline_mode=pl.Buffered(1) if resident else None)


def _in_proj(x2d, gain, w_in, conv_w, layer, seq):
    n_rows, d_model = x2d.shape
    tm = ROW_TILE
    assert n_rows % tm == 0 and seq % tm == 0 and tm % SUB_ROWS == 0
    row_spec = lambda width: pl.BlockSpec((tm, width), lambda i: (i, 0))
    out = jax.ShapeDtypeStruct((n_rows, PART), BF16)
    return pl.pallas_call(
        functools.partial(_in_proj_kernel, seq // tm),
        grid=(n_rows // tm,),
        in_specs=[row_spec(d_model), _layer_spec(gain, layer),
                  _layer_spec(w_in, layer, resident=True), _layer_spec(conv_w, layer)],
        out_specs=[row_spec(PART)] * 5,
        out_shape=[out] * 5,
        scratch_shapes=[pltpu.VMEM(w_in.shape[1:], BF16), pltpu.VMEM((SUBLANES, PART), F32)],
        compiler_params=pltpu.CompilerParams(
            dimension_semantics=("arbitrary",), vmem_limit_bytes=VMEM_LIMIT_BYTES),
        name="in_proj",
    )(x2d, gain, w_in, conv_w)


def _split_bf16(x64, n):
    parts, r = [], np.array(x64, dtype=np.float64)
    for _ in range(n):
        p = r.astype(np.float32).astype(BF16)
        parts.append(p)
        r = r - p.astype(np.float64)
    return parts


@functools.lru_cache(maxsize=None)
def _alibi_tables(seq, tile):
    slopes = 2.0 ** (-8.0 * np.arange(1, N_HEADS + 1) / N_HEADS)
    pos = np.arange(seq, dtype=np.float64)
    qf = np.zeros((N_HEADS, seq, LANES), dtype=BF16)
    kf = np.zeros((N_HEADS, seq, LANES), dtype=BF16)
    r = np.arange(tile)
    diag = np.zeros((N_HEADS, tile, tile), dtype=np.float32)
    for h, m in enumerate(slopes):
        for n, piece in enumerate(_split_bf16(-m * LOG2E * pos, N_SPLIT)):
            qf[h, :, n] = piece
        qf[h, :, N_SPLIT:2 * N_SPLIT] = 1.0
        kf[h, :, 0:N_SPLIT] = 1.0
        for n, piece in enumerate(_split_bf16(m * LOG2E * pos, N_SPLIT)):
            kf[h, :, N_SPLIT + n] = piece
        ahead = r[None, :] - r[:, None]
        same_chunk = (r[None, :] // CHUNK) == (r[:, None] // CHUNK)
        later_chunk = (r[None, :] // CHUNK) > (r[:, None] // CHUNK)
        d = np.where((ahead > 0) & same_chunk, -2.0 * m * LOG2E * ahead, 0.0)
        diag[h] = np.where(later_chunk, MASK_VALUE, d).astype(np.float32)
    return qf, kf, diag


def _diff_attn_kernel(lambda_init, q_ref, k_ref, v_ref, ga_ref, qf_ref, kf_ref, diag_ref,
                      lam_ref, sg_ref, o_ref, s_ref, e_ref):
    seq = q_ref.shape[0]
    t = diag_ref.shape[0]
    n_tiles = seq // t
    tile = lambda ref, j: ref[j * t:(j + 1) * t, :]

    lp = lam_ref[...]
    lam = (jnp.exp(jnp.sum(lp[0:1] * lp[1:2], axis=-1, keepdims=True))
           - jnp.exp(jnp.sum(lp[2:3] * lp[3:4], axis=-1, keepdims=True)) + lambda_init)
    out_gain = sg_ref[...] * (1.0 - lambda_init)
    d = diag_ref[...]
    dd = jnp.concatenate([d, d], axis=0)
    ones = jnp.ones((t, LANES), BF16)

    def fold_max(s):
        m = s[:, :LANES]
        for c in range(1, t // LANES):
            m = jnp.maximum(m, s[:, c * LANES:(c + 1) * LANES])
        return m

    def pass1(i, slot):
        q = tile(q_ref, i)
        qf = tile(qf_ref, i)
        lane = lax.broadcasted_iota(jnp.int32, q.shape, 1)
        zero = jnp.zeros_like(q)
        qaug = jnp.concatenate(
            [jnp.concatenate([jnp.where(lane < HEAD_DIM_QK, q, zero), qf], axis=1),
             jnp.concatenate([jnp.where(lane >= HEAD_DIM_QK, q, zero), qf], axis=1)], axis=0)
        m = None
        for j in range(i + 1):
            kt = jnp.concatenate([tile(k_ref, j), tile(kf_ref, j)], axis=1)
            s = lax.dot_general(qaug, kt, (((1,), (1,)), ((), ())), preferred_element_type=F32)
            if j == i:
                s = s + dd
            s_ref[slot, :, j * t:(j + 1) * t] = s
            f = fold_max(s)
            m = f if m is None else jnp.maximum(m, f)
        return jnp.max(m, axis=-1, keepdims=True)

    def pass2(i, slot, m):
        for j in range(i + 1):
            e_ref[slot, :, j * t:(j + 1) * t] = jnp.exp2(s_ref[slot, :, j * t:(j + 1) * t] - m).astype(BF16)
        k_end = (i + 1) * t
        vaug = jnp.concatenate([v_ref[:k_end, :], jnp.concatenate([ones] * (i + 1), axis=0)], axis=1)
        acc = jnp.dot(e_ref[slot, :, :k_end], vaug, preferred_element_type=F32)
        o = acc[:t, :LANES] / acc[:t, LANES:] - lam * (acc[t:, :LANES] / acc[t:, LANES:])
        o = _rms(o, SUBLN_EPS) * out_gain
        o_ref[i * t:(i + 1) * t, :] = (o * tile(ga_ref, i).astype(F32)).astype(BF16)

    order = list(range(1, n_tiles, 2)) + list(range(n_tiles - 2 + n_tiles % 2, -1, -2))
    assert sorted(order) == list(range(n_tiles))
    row_max = {}
    for n in range(n_tiles + 1):
        if n < n_tiles:
            row_max[n] = pass1(order[n], n % ATT_SLOTS)
        if n >= 1:
            pass2(order[n - 1], (n - 1) % ATT_SLOTS, row_max.pop(n - 1))


def _diff_attn(q, k, v, ga, lam_params, subln_gain, layer, lambda_init):
    bsz, seq, _ = q.shape
    t = ATT_TILE
    assert seq % t == 0 and t % CHUNK == 0
    qf, kf, diag = _alibi_tables(seq, t)
    head_spec = pl.BlockSpec((None, seq, LANES), lambda b, h: (b, 0, h))
    table_spec = pl.BlockSpec((None, seq, LANES), lambda b, h: (h, 0, 0))
    return pl.pallas_call(
        functools.partial(_diff_attn_kernel, lambda_init),
        grid=(bsz, N_HEADS),
        in_specs=[head_spec, head_spec, head_spec, head_spec, table_spec, table_spec,
                  pl.BlockSpec((None, t, t), lambda b, h: (h, 0, 0)),
                  _layer_spec(lam_params, layer), _layer_spec(subln_gain, layer)],
        out_specs=head_spec,
        out_shape=jax.ShapeDtypeStruct((bsz, seq, N_HEADS * HEAD_DIM_V), BF16),
        scratch_shapes=[pltpu.VMEM((ATT_SLOTS, 2 * t, seq), F32),
                        pltpu.VMEM((ATT_SLOTS, 2 * t, seq), BF16)],
        compiler_params=pltpu.CompilerParams(
            dimension_semantics=("arbitrary", "arbitrary"), vmem_limit_bytes=VMEM_LIMIT_BYTES),
        name="diff_attn",
    )(q, k, v, ga, qf, kf, diag, lam_params, subln_gain)


def _out_mix_kernel(a_ref, c_ref, x_ref, p_ref, wo32_ref, pg_ref, wp32_ref, eg_ref, wg32_ref,
                    o_ref, wo_ref, wp_ref, wg_ref):
    d_attn = a_ref.shape[1]
    n_sub = x_ref.shape[0] // MIX_SUB_ROWS
    sub = lambda r: slice(r * MIX_SUB_ROWS, (r + 1) * MIX_SUB_ROWS)

    @pl.when(pl.program_id(0) == 0)
    def _():
        _cast_weight(wo_ref, wo32_ref)
        _cast_weight(wp_ref, wp32_ref)
        _cast_weight(wg_ref, wg32_ref)

    def mix(r):
        rows = sub(r)
        mixed = (jnp.dot(a_ref[rows, :], wo_ref[:d_attn, :], preferred_element_type=F32)
                 + jnp.dot(c_ref[rows, :], wo_ref[d_attn:, :], preferred_element_type=F32))
        x1 = x_ref[rows, :] + _rms(mixed, EPS) * pg_ref[...]
        e = _rms(jnp.dot(p_ref[rows, :].astype(BF16), wp_ref[...], preferred_element_type=F32),
                 EPS) * eg_ref[...]
        return x1, e

    def gate(r, x1, e):
        g = jax.nn.sigmoid(jnp.dot(x1.astype(BF16), wg_ref[...], preferred_element_type=F32))
        o_ref[sub(r), :] = x1 + g * e

    cur = mix(0)
    for r in range(n_sub):
        nxt = mix(r + 1) if r + 1 < n_sub else None
        gate(r, *cur)
        cur = nxt


def _out_mix(attn2d, conv2d, x2d, p3d, w_out, post_gain, w_ple, ple_gain, w_gate, layer):
    n_rows, d_model = x2d.shape
    tm = ROW_TILE
    row_spec = lambda a: pl.BlockSpec((tm, a.shape[1]), lambda i: (i, 0))
    weights = (w_out, w_ple, w_gate)
    return pl.pallas_call(
        _out_mix_kernel,
        grid=(n_rows // tm,),
        in_specs=[row_spec(attn2d), row_spec(conv2d), row_spec(x2d),
                  pl.BlockSpec((None, tm, p3d.shape[2]), lambda i: (layer, i, 0)),
                  _layer_spec(w_out, layer, resident=True), _layer_spec(post_gain, layer),
                  _layer_spec(w_ple, layer, resident=True), _layer_spec(ple_gain, layer),
                  _layer_spec(w_gate, layer, resident=True)],
        out_specs=pl.BlockSpec((tm, d_model), lambda i: (i, 0)),
        out_shape=jax.ShapeDtypeStruct((n_rows, d_model), F32),
        scratch_shapes=[pltpu.VMEM(w.shape[1:], BF16) for w in weights],
        compiler_params=pltpu.CompilerParams(
            dimension_semantics=("arbitrary",), vmem_limit_bytes=VMEM_LIMIT_BYTES),
        name="out_mix",
    )(attn2d, conv2d, x2d, p3d, w_out, post_gain, w_ple, ple_gain, w_gate)


def kernel(x, p, pre_norm_gain, w_in, lambda_q1, lambda_k1, lambda_q2, lambda_k2, subln_gain,
           conv_w, w_out, post_norm_gain, w_ple_proj, ple_norm_gain, w_ple_gate):
    bsz, seq, d_model = x.shape
    depth = p.shape[0]
    n_rows = bsz * seq
    x2d = x.reshape(n_rows, d_model)
    p3d = p.reshape(depth, n_rows, p.shape[-1])
    row_vec = lambda a: a[:, None, :]
    lam_params = jnp.stack([lambda_q1, lambda_k1, lambda_q2, lambda_k2], axis=1)
    to3d = lambda a: a.reshape(bsz, seq, PART)
    for i in range(depth):
        lambda_init = 0.8 - 0.6 * math.exp(-0.3 * i)
        q, k, v, ga, co = _in_proj(x2d, row_vec(pre_norm_gain), w_in, conv_w, i, seq)
        attn = _diff_attn(to3d(q), to3d(k), to3d(v), to3d(ga), lam_params, row_vec(subln_gain),
                          i, lambda_init)
        x2d = _out_mix(attn.reshape(n_rows, PART), co, x2d, p3d, w_out, row_vec(post_norm_gain),
                       w_ple_proj, row_vec(ple_norm_gain), w_ple_gate, i)
    return x2d.reshape(bsz, seq, d_model)
```

```python
import functools
import math

import numpy as np
import jax
import jax.numpy as jnp
from jax import lax
from jax.experimental import pallas as pl
from jax.experimental.pallas import tpu as pltpu

F32 = jnp.float32
BF16 = jnp.bfloat16

CHUNK = 64
N_HEADS = 4
HEAD_DIM_V = 128
HEAD_DIM_QK = 64
PART = 512
CONV_WIDTH = 3
EPS = 1e-6
SUBLN_EPS = 1e-5
LOG2E = 1.4426950408889634
MASK_VALUE = -1e30

LANES = 128
SUBLANES = 8
VMEM_LIMIT_BYTES = 56 * 1024 * 1024

ROW_TILE = 1024
SUB_ROWS = 256
MIX_SUB_ROWS = 512
ATT_TILE = 256
ATT_HEADS = 2
ATT_SKEW = 1
ATT_SLOTS = ATT_SKEW + 1
N_SPLIT = 4


def _rms(y, eps):
    return y * lax.rsqrt(jnp.mean(y * y, axis=-1, keepdims=True) + eps)


def _silu(z):
    return z * jax.nn.sigmoid(z)


def _cast_weight(dst_ref, src_ref):
    for c in range(0, src_ref.shape[1], PART):
        dst_ref[:, c:c + PART] = src_ref[:, c:c + PART].astype(BF16)


def _layer_spec(a, layer, resident=False):
    return pl.BlockSpec((None,) + a.shape[1:], lambda *_: (layer,) + (0,) * (a.ndim - 1),
                        pipeline_mode=pl.Buffered(1) if resident else None)


def _in_proj_kernel(tiles_per_seq, x_ref, gain_ref, w32_ref, cw_ref,
                    q_ref, k_ref, v_ref, ga_ref, co_ref, w_ref, carry_ref):
    tm = x_ref.shape[0]
    n_sub = tm // SUB_ROWS
    sub = lambda r: slice(r * SUB_ROWS, (r + 1) * SUB_ROWS)

    @pl.when(pl.program_id(0) == 0)
    def _():
        _cast_weight(w_ref, w32_ref)

    @pl.when(pl.program_id(0) % tiles_per_seq == 0)
    def _():
        carry_ref[...] = jnp.zeros_like(carry_ref)

    def norm(r):
        return (_rms(x_ref[sub(r), :], EPS) * gain_ref[...]).astype(BF16)

    def project(r, h, tail):
        rows = sub(r)

        def part(n):
            return jnp.dot(h, w_ref[:, n * PART:(n + 1) * PART], preferred_element_type=F32)

        b_gate = part(4)
        g = part(5) * part(6)
        row = lax.broadcasted_iota(jnp.int32, g.shape, 0)
        prev1 = tail[SUBLANES - 1:SUBLANES, :]
        prev2 = tail[SUBLANES - 2:SUBLANES - 1, :]
        g1 = jnp.where(row == 0, prev1, pltpu.roll(g, 1, axis=0))
        g2 = pltpu.roll(g, 2, axis=0)
        g2 = jnp.where(row == 0, prev2, jnp.where(row == 1, prev1, g2))
        cw = cw_ref[...]
        conv = cw[0:1, :] * g2 + cw[1:2, :] * g1 + cw[2:3, :] * g
        co_ref[rows, :] = (b_gate * conv * _silu(part(7))).astype(BF16)
        ga_ref[rows, :] = _silu(part(3)).astype(BF16)
        q_ref[rows, :] = (part(0) * (LOG2E * HEAD_DIM_QK ** -0.5)).astype(BF16)
        k_ref[rows, :] = part(1).astype(BF16)
        v_ref[rows, :] = part(2).astype(BF16)
        return g[SUB_ROWS - SUBLANES:, :]

    tail = carry_ref[...]
    h = norm(0)
    for r in range(n_sub):
        h_next = norm(r + 1) if r + 1 < n_sub else None
        tail = project(r, h, tail)
        h = h_next
    carry_ref[...] = tail


def _in_proj(x2d, gain, w_in, conv_w, layer, seq):
    n_rows, d_model = x2d.shape
    tm = ROW_TILE
    assert n_rows % tm == 0 and seq % tm == 0 and tm % SUB_ROWS == 0
    row_spec = lambda width: pl.BlockSpec((tm, width), lambda i: (i, 0))
    out = jax.ShapeDtypeStruct((n_rows, PART), BF16)
    return pl.pallas_call(
        functools.partial(_in_proj_kernel, seq // tm),
        grid=(n_rows // tm,),
        in_specs=[row_spec(d_model), _layer_spec(gain, layer),
                  _layer_spec(w_in, layer, resident=True), _layer_spec(conv_w, layer)],
        out_specs=[row_spec(PART)] * 5,
        out_shape=[out] * 5,
        scratch_shapes=[pltpu.VMEM(w_in.shape[1:], BF16), pltpu.VMEM((SUBLANES, PART), F32)],
        compiler_params=pltpu.CompilerParams(
            dimension_semantics=("arbitrary",), vmem_limit_bytes=VMEM_LIMIT_BYTES),
        name="in_proj",
    )(x2d, gain, w_in, conv_w)


def _split_bf16(x64, n):
    parts, r = [], np.array(x64, dtype=np.float64)
    for _ in range(n):
        p = r.astype(np.float32).astype(BF16)
        parts.append(p)
        r = r - p.astype(np.float64)
    return parts


@functools.lru_cache(maxsize=None)
def _alibi_tables(seq, tile):
    slopes = 2.0 ** (-8.0 * np.arange(1, N_HEADS + 1) / N_HEADS)
    pos = np.arange(seq, dtype=np.float64)
    qf = np.zeros((N_HEADS, seq, LANES), dtype=BF16)
    kf = np.zeros((N_HEADS, seq, LANES), dtype=BF16)
    r = np.arange(tile)
    diag = np.zeros((N_HEADS, tile, tile), dtype=np.float32)
    for h, m in enumerate(slopes):
        for n, piece in enumerate(_split_bf16(-m * LOG2E * pos, N_SPLIT)):
            qf[h, :, n] = piece
        qf[h, :, N_SPLIT:2 * N_SPLIT] = 1.0
        kf[h, :, 0:N_SPLIT] = 1.0
        for n, piece in enumerate(_split_bf16(m * LOG2E * pos, N_SPLIT)):
            kf[h, :, N_SPLIT + n] = piece
        ahead = r[None, :] - r[:, None]
        same_chunk = (r[None, :] // CHUNK) == (r[:, None] // CHUNK)
        later_chunk = (r[None, :] // CHUNK) > (r[:, None] // CHUNK)
        d = np.where((ahead > 0) & same_chunk, -2.0 * m * LOG2E * ahead, 0.0)
        diag[h] = np.where(later_chunk, MASK_VALUE, d).astype(np.float32)
    return qf, kf, diag


def _diff_attn_kernel(lambda_init, q_ref, k_ref, v_ref, ga_ref, qf_ref, kf_ref, diag_ref,
                      lam_ref, sg_ref, o_ref, s_ref, e_ref):
    seq = q_ref.shape[0]
    t = diag_ref.shape[1]
    n_tiles = seq // t
    tile = lambda ref, hd, j: ref[j * t:(j + 1) * t, hd * LANES:(hd + 1) * LANES]

    lp = lam_ref[...]
    lam = (jnp.exp(jnp.sum(lp[0:1] * lp[1:2], axis=-1, keepdims=True))
           - jnp.exp(jnp.sum(lp[2:3] * lp[3:4], axis=-1, keepdims=True)) + lambda_init)
    out_gain = sg_ref[...] * (1.0 - lambda_init)
    ones = jnp.ones((t, LANES), BF16)

    def fold_max(s):
        m = s[:, :LANES]
        for c in range(1, t // LANES):
            m = jnp.maximum(m, s[:, c * LANES:(c + 1) * LANES])
        return m

    def pass1(hd, i, slot):
        q = tile(q_ref, hd, i)
        qf = qf_ref[hd, i * t:(i + 1) * t, :]
        lane = lax.broadcasted_iota(jnp.int32, q.shape, 1)
        zero = jnp.zeros_like(q)
        qaug = jnp.concatenate(
            [jnp.concatenate([jnp.where(lane < HEAD_DIM_QK, q, zero), qf], axis=1),
             jnp.concatenate([jnp.where(lane >= HEAD_DIM_QK, q, zero), qf], axis=1)], axis=0)
        m = None
        for j in range(i + 1):
            kt = jnp.concatenate([tile(k_ref, hd, j), kf_ref[hd, j * t:(j + 1) * t, :]], axis=1)
            s = lax.dot_general(qaug, kt, (((1,), (1,)), ((), ())), preferred_element_type=F32)
            if j == i:
                d = diag_ref[hd]
                s = s + jnp.concatenate([d, d], axis=0)
            s_ref[slot, :, j * t:(j + 1) * t] = s
            f = fold_max(s)
            m = f if m is None else jnp.maximum(m, f)
        return jnp.max(m, axis=-1, keepdims=True)

    def pass2(hd, i, slot, m):
        for j in range(i + 1):
            e_ref[slot, :, j * t:(j + 1) * t] = jnp.exp2(s_ref[slot, :, j * t:(j + 1) * t] - m).astype(BF16)
        k_end = (i + 1) * t
        vaug = jnp.concatenate([v_ref[:k_end, hd * LANES:(hd + 1) * LANES],
                                jnp.concatenate([ones] * (i + 1), axis=0)], axis=1)
        acc = jnp.dot(e_ref[slot, :, :k_end], vaug, preferred_element_type=F32)
        o = acc[:t, :LANES] / acc[:t, LANES:] - lam * (acc[t:, :LANES] / acc[t:, LANES:])
        o = _rms(o, SUBLN_EPS) * out_gain
        o_ref[i * t:(i + 1) * t, hd * LANES:(hd + 1) * LANES] = (
            o * tile(ga_ref, hd, i).astype(F32)).astype(BF16)

    order = list(range(1, n_tiles, 2)) + list(range(n_tiles - 2 + n_tiles % 2, -1, -2))
    assert sorted(order) == list(range(n_tiles))
    work = [(hd, i) for hd in range(ATT_HEADS) for i in order]
    row_max = {}
    for n in range(len(work) + ATT_SKEW):
        if n < len(work):
            row_max[n] = pass1(*work[n], n % ATT_SLOTS)
        if n >= ATT_SKEW:
            pass2(*work[n - ATT_SKEW], (n - ATT_SKEW) % ATT_SLOTS, row_max.pop(n - ATT_SKEW))


def _diff_attn(q, k, v, ga, lam_params, subln_gain, layer, lambda_init):
    bsz, seq, _ = q.shape
    t = ATT_TILE
    assert seq % t == 0 and t % CHUNK == 0
    qf, kf, diag = _alibi_tables(seq, t)
    assert N_HEADS % ATT_HEADS == 0
    head_spec = pl.BlockSpec((None, seq, ATT_HEADS * LANES), lambda b, h: (b, 0, h))
    table_spec = pl.BlockSpec((ATT_HEADS, seq, LANES), lambda b, h: (h, 0, 0))
    return pl.pallas_call(
        functools.partial(_diff_attn_kernel, lambda_init),
        grid=(bsz, N_HEADS // ATT_HEADS),
        in_specs=[head_spec, head_spec, head_spec, head_spec, table_spec, table_spec,
                  pl.BlockSpec((ATT_HEADS, t, t), lambda b, h: (h, 0, 0)),
                  _layer_spec(lam_params, layer), _layer_spec(subln_gain, layer)],
        out_specs=head_spec,
        out_shape=jax.ShapeDtypeStruct((bsz, seq, N_HEADS * HEAD_DIM_V), BF16),
        scratch_shapes=[pltpu.VMEM((ATT_SLOTS, 2 * t, seq), F32),
                        pltpu.VMEM((ATT_SLOTS, 2 * t, seq), BF16)],
        compiler_params=pltpu.CompilerParams(
            dimension_semantics=("arbitrary", "arbitrary"), vmem_limit_bytes=VMEM_LIMIT_BYTES),
        name="diff_attn",
    )(q, k, v, ga, qf, kf, diag, lam_params, subln_gain)


def _out_mix_kernel(a_ref, c_ref, x_ref, p_ref, wo32_ref, pg_ref, wp32_ref, eg_ref, wg32_ref,
                    o_ref, wo_ref, wp_ref, wg_ref):
    d_attn = a_ref.shape[1]
    n_sub = x_ref.shape[0] // MIX_SUB_ROWS
    sub = lambda r: slice(r * MIX_SUB_ROWS, (r + 1) * MIX_SUB_ROWS)

    @pl.when(pl.program_id(0) == 0)
    def _():
        _cast_weight(wo_ref, wo32_ref)
        _cast_weight(wp_ref, wp32_ref)
        _cast_weight(wg_ref, wg32_ref)

    def mix(r):
        rows = sub(r)
        mixed = (jnp.dot(a_ref[rows, :], wo_ref[:d_attn, :], preferred_element_type=F32)
                 + jnp.dot(c_ref[rows, :], wo_ref[d_attn:, :], preferred_element_type=F32))
        x1 = x_ref[rows, :] + _rms(mixed, EPS) * pg_ref[...]
        e = _rms(jnp.dot(p_ref[rows, :].astype(BF16), wp_ref[...], preferred_element_type=F32),
                 EPS) * eg_ref[...]
        return x1, e

    def gate(r, x1, e):
        g = jax.nn.sigmoid(jnp.dot(x1.astype(BF16), wg_ref[...], preferred_element_type=F32))
        o_ref[sub(r), :] = x1 + g * e

    cur = mix(0)
    for r in range(n_sub):
        nxt = mix(r + 1) if r + 1 < n_sub else None
        gate(r, *cur)
        cur = nxt


def _out_mix(attn2d, conv2d, x2d, p3d, w_out, post_gain, w_ple, ple_gain, w_gate, layer):
    n_rows, d_model = x2d.shape
    tm = ROW_TILE
    row_spec = lambda a: pl.BlockSpec((tm, a.shape[1]), lambda i: (i, 0))
    weights = (w_out, w_ple, w_gate)
    return pl.pallas_call(
        _out_mix_kernel,
        grid=(n_rows // tm,),
        in_specs=[row_spec(attn2d), row_spec(conv2d), row_spec(x2d),
                  pl.BlockSpec((None, tm, p3d.shape[2]), lambda i: (layer, i, 0)),
                  _layer_spec(w_out, layer, resident=True), _layer_spec(post_gain, layer),
                  _layer_spec(w_ple, layer, resident=True), _layer_spec(ple_gain, layer),
                  _layer_spec(w_gate, layer, resident=True)],
        out_specs=pl.BlockSpec((tm, d_model), lambda i: (i, 0)),
        out_shape=jax.ShapeDtypeStruct((n_rows, d_model), F32),
        scratch_shapes=[pltpu.VMEM(w.shape[1:], BF16) for w in weights],
        compiler_params=pltpu.CompilerParams(
            dimension_semantics=("arbitrary",), vmem_limit_bytes=VMEM_LIMIT_BYTES),
        name="out_mix",
    )(attn2d, conv2d, x2d, p3d, w_out, post_gain, w_ple, ple_gain, w_gate)


def kernel(x, p, pre_norm_gain, w_in, lambda_q1, lambda_k1, lambda_q2, lambda_k2, subln_gain,
           conv_w, w_out, post_norm_gain, w_ple_proj, ple_norm_gain, w_ple_gate):
    bsz, seq, d_model = x.shape
    depth = p.shape[0]
    n_rows = bsz * seq
    x2d = x.reshape(n_rows, d_model)
    p3d = p.reshape(depth, n_rows, p.shape[-1])
    row_vec = lambda a: a[:, None, :]
    lam_params = jnp.stack([lambda_q1, lambda_k1, lambda_q2, lambda_k2], axis=1)
    to3d = lambda a: a.reshape(bsz, seq, PART)
    for i in range(depth):
        lambda_init = 0.8 - 0.6 * math.exp(-0.3 * i)
        q, k, v, ga, co = _in_proj(x2d, row_vec(pre_norm_gain), w_in, conv_w, i, seq)
        attn = _diff_attn(to3d(q), to3d(k), to3d(v), to3d(ga), lam_params, row_vec(subln_gain),
                          i, lambda_init)
        x2d = _out_mix(attn.reshape(n_rows, PART), co, x2d, p3d, w_out, row_vec(post_norm_gain),
                       w_ple_proj, row_vec(ple_norm_gain), w_ple_gate, i)
    return x2d.reshape(bsz, seq, d_model)
```
